```python
import math
import jax, jax.numpy as jnp
from jax import lax
import numpy as np

D_MODEL = 2048
BATCH = 2
SEQ = 4096
DEPTH = 4
DEC_BATCH = 8
DEC_SEQ = 8
PAST_LEN = 16384
PAGE_SIZE = 128

HEAD_DIM = 128
N_HEADS = D_MODEL // HEAD_DIM
BRANCH = N_HEADS * HEAD_DIM
CONV_W = 3
MOBA_BLOCK = 256
MOBA_TOPK = 3
MOBA_QCHUNK = 16
SB_QCHUNK = 128
REL_BUCKETS = 32
REL_EXACT = REL_BUCKETS // 2
REL_MAX_DIST = 128
NORM_EPS = 1e-6
MOBA, SHORTCONV, STICKBREAK = 0, 1, 2
N_MIXERS = 3
MIXER_OF_LAYER = tuple(i % N_MIXERS for i in range(DEPTH))
ATTN_LAYERS = tuple(i for i in range(DEPTH) if MIXER_OF_LAYER[i] != SHORTCONV)
CONV_LAYERS = tuple(i for i in range(DEPTH) if MIXER_OF_LAYER[i] == SHORTCONV)
N_ATTN_LAYERS = len(ATTN_LAYERS)
N_CONV_LAYERS = len(CONV_LAYERS)

kernel_name = 'hybrid_moba_shortconv_stickbreak_step'

F32 = jnp.float32


def rmsnorm(x, g):
    xf = x.astype(F32)
    y = xf * lax.rsqrt(jnp.mean(xf * xf, axis=-1, keepdims=True) + NORM_EPS)
    return (y * g.astype(F32)).astype(x.dtype)


def rel_bucket(dist):
    n = jnp.maximum(dist, 0)
    nf = jnp.maximum(n, REL_EXACT).astype(F32)
    large = REL_EXACT + (jnp.log(nf / REL_EXACT)
                         * ((REL_BUCKETS - REL_EXACT) / math.log(REL_MAX_DIST / REL_EXACT))).astype(jnp.int32)
    return jnp.where(n < REL_EXACT, n, jnp.minimum(large, REL_BUCKETS - 1))


def heads(t):
    return t.reshape(t.shape[0], t.shape[1], N_HEADS, HEAD_DIM)


def own_block_scores(q, k_own, qpos, kpos_own, rel_bias):
    s = jnp.einsum('nqhd,nkhd->nqhk', q, k_own, preferred_element_type=F32) * (HEAD_DIM ** -0.5)
    bias = rel_bias[rel_bucket(qpos[:, None] - kpos_own[None, :])].astype(F32)
    s = s + bias.transpose(0, 2, 1)[None]
    causal = (kpos_own[None, :] <= qpos[:, None])[None, :, None, :]
    return jnp.where(causal, s, -jnp.inf)


def selected_block_scores(q, k_sel, qpos, kpos_sel, rel_bias):
    s = jnp.einsum('nqhd,nqhkd->nqhk', q, k_sel, preferred_element_type=F32) * (HEAD_DIM ** -0.5)
    h_idx = jnp.arange(N_HEADS)[None, None, :, None]
    bucket = rel_bucket(qpos[None, :, None, None] - kpos_sel)
    return s + rel_bias.T.astype(F32)[h_idx, bucket]


def moba_prompt(q, k, v, rel_bias):
    nb, S, H, Dh = q.shape
    n_blk = -(-S // MOBA_BLOCK)
    pad = n_blk * MOBA_BLOCK - S
    kb = jnp.pad(k, ((0, 0), (0, pad), (0, 0), (0, 0))).reshape(nb, n_blk, MOBA_BLOCK, H, Dh)
    vb = jnp.pad(v, ((0, 0), (0, pad), (0, 0), (0, 0))).reshape(nb, n_blk, MOBA_BLOCK, H, Dh)
    k_mean = jnp.mean(kb.astype(F32), axis=2)
    kbh = kb.transpose(0, 3, 1, 2, 4)
    vbh = vb.transpose(0, 3, 1, 2, 4)
    n_sel = min(MOBA_TOPK, n_blk - 1)
    b_idx = jnp.arange(nb)[:, None, None, None]
    h_idx = jnp.arange(H)[None, None, :, None]
    blk_ar = jnp.arange(MOBA_BLOCK)
    n_chunks = S // MOBA_QCHUNK
    q_chunks = q.reshape(nb, n_chunks, MOBA_QCHUNK, H, Dh).swapaxes(0, 1)

    def chunk(args):
        qi, c = args
        q0 = c * MOBA_QCHUNK
        qpos = q0 + jnp.arange(MOBA_QCHUNK)
        qblk = q0 // MOBA_BLOCK
        k_own = lax.dynamic_index_in_dim(kb, qblk, axis=1, keepdims=False)
        v_own = lax.dynamic_index_in_dim(vb, qblk, axis=1, keepdims=False)
        s_own = own_block_scores(qi, k_own, qpos, qblk * MOBA_BLOCK + blk_ar, rel_bias)
        if n_sel == 0:
            p = jax.nn.softmax(s_own, axis=-1).astype(v.dtype)
            return jnp.einsum('bqhk,bkhd->bqhd', p, v_own)
        gate = jnp.einsum('bqhd,bnhd->bqhn', qi.astype(F32), k_mean)
        gate = jnp.where(jnp.arange(n_blk) < qblk, gate, -jnp.inf)
        _, idx = lax.top_k(gate, n_sel)
        k_sel = kbh[b_idx, h_idx, idx].reshape(nb, MOBA_QCHUNK, H, n_sel * MOBA_BLOCK, Dh)
        v_sel = vbh[b_idx, h_idx, idx].reshape(nb, MOBA_QCHUNK, H, n_sel * MOBA_BLOCK, Dh)
        kpos_sel = (idx[..., None] * MOBA_BLOCK + blk_ar).reshape(nb, MOBA_QCHUNK, H, n_sel * MOBA_BLOCK)
        s_sel = selected_block_scores(qi, k_sel, qpos, kpos_sel, rel_bias)
        s_sel = jnp.where(jnp.repeat(idx < qblk, MOBA_BLOCK, axis=-1), s_sel, -jnp.inf)
        p = jax.nn.softmax(jnp.concatenate([s_sel, s_own], axis=-1), axis=-1).astype(v.dtype)
        return (jnp.einsum('bqhk,bqhkd->bqhd', p[..., :n_sel * MOBA_BLOCK], v_sel)
                + jnp.einsum('bqhk,bkhd->bqhd', p[..., n_sel * MOBA_BLOCK:], v_own))

    out = lax.map(chunk, (q_chunks, jnp.arange(n_chunks)))
    return out.swapaxes(0, 1).reshape(nb, S, H, Dh)


def moba_sample(q, k_new, v_new, cache_k, cache_v, j, page_table, rel_bias):
    n, T, H, Dh = q.shape
    n_pages = page_table.shape[1]
    past = n_pages * PAGE_SIZE
    ppb = MOBA_BLOCK // PAGE_SIZE
    nb_past = past // MOBA_BLOCK
    own_start = nb_past * MOBA_BLOCK
    n_part_pages = n_pages - own_start // PAGE_SIZE
    n_sel = min(MOBA_TOPK, nb_past)
    qpos = past + jnp.arange(T)
    if n_part_pages > 0:
        part = page_table[:, own_start // PAGE_SIZE:]
        k_own = jnp.concatenate([cache_k[j, part].reshape(n, n_part_pages * PAGE_SIZE, H, Dh), k_new], axis=1)
        v_own = jnp.concatenate([cache_v[j, part].reshape(n, n_part_pages * PAGE_SIZE, H, Dh), v_new], axis=1)
    else:
        k_own, v_own = k_new, v_new
    s_own = own_block_scores(q, k_own, qpos, own_start + jnp.arange(k_own.shape[1]), rel_bias)
    if n_sel == 0:
        p = jax.nn.softmax(s_own, axis=-1).astype(v_new.dtype)
        return jnp.einsum('nthk,nkhd->nthd', p, v_own)
    k_full = cache_k[j, page_table[:, :nb_past * ppb]]
    k_mean = jnp.mean(k_full.reshape(n, nb_past, MOBA_BLOCK, H, Dh).astype(F32), axis=2)
    gate = jnp.einsum('nthd,nbhd->nthb', q.astype(F32), k_mean)
    _, idx = lax.top_k(gate, n_sel)
    logical = idx[..., None] * ppb + jnp.arange(ppb)
    phys = page_table[jnp.arange(n)[:, None, None, None, None], logical]
    h5 = jnp.arange(H)[None, None, :, None, None]
    k_sel = cache_k[j, phys, :, h5, :].reshape(n, T, H, n_sel * MOBA_BLOCK, Dh)
    v_sel = cache_v[j, phys, :, h5, :].reshape(n, T, H, n_sel * MOBA_BLOCK, Dh)
    kpos_sel = (idx[..., None] * MOBA_BLOCK + jnp.arange(MOBA_BLOCK)).reshape(n, T, H, n_sel * MOBA_BLOCK)
    s_sel = selected_block_scores(q, k_sel, qpos, kpos_sel, rel_bias)
    p = jax.nn.softmax(jnp.concatenate([s_sel, s_own], axis=-1), axis=-1).astype(v_new.dtype)
    return (jnp.einsum('nthk,nthkd->nthd', p[..., :n_sel * MOBA_BLOCK], v_sel)
            + jnp.einsum('nthk,nkhd->nthd', p[..., n_sel * MOBA_BLOCK:], v_own))


def stick_breaking_weights(z, visible):
    log_beta = jax.nn.log_sigmoid(z)
    log_keep = jnp.where(visible, jax.nn.log_sigmoid(-z), 0.0)
    after = lax.cumsum(log_keep, axis=z.ndim - 1, reverse=True) - log_keep
    return jnp.where(visible, jnp.exp(log_beta + after), 0.0)


def stickbreak_prompt(q, k, v):
    nb, S, H, Dh = q.shape
    n_chunks = S // SB_QCHUNK
    q_chunks = q.reshape(nb, n_chunks, SB_QCHUNK, H, Dh).swapaxes(0, 1)
    kpos = jnp.arange(S)

    def chunk(args):
        qi, c = args
        qpos = c * SB_QCHUNK + jnp.arange(SB_QCHUNK)
        z = jnp.einsum('bqhd,bkhd->bhqk', qi, k, preferred_element_type=F32) * (Dh ** -0.5)
        a = stick_breaking_weights(z, kpos[None, :] < qpos[:, None])
        return jnp.einsum('bhqk,bkhd->bqhd', a.astype(v.dtype), v)

    out = lax.map(chunk, (q_chunks, jnp.arange(n_chunks)))
    return out.swapaxes(0, 1).reshape(nb, S, H, Dh)


def stickbreak_sample(q, k_new, v_new, cache_k, cache_v, j, page_table):
    n, T, H, Dh = q.shape
    past = page_table.shape[1] * PAGE_SIZE
    k_all = jnp.concatenate([cache_k[j, page_table].reshape(n, past, H, Dh), k_new], axis=1)
    v_all = jnp.concatenate([cache_v[j, page_table].reshape(n, past, H, Dh), v_new], axis=1)
    qpos = past + jnp.arange(T)
    kpos = jnp.arange(past + T)
    z = jnp.einsum('nthd,nkhd->nhtk', q, k_all, preferred_element_type=F32) * (Dh ** -0.5)
    a = stick_breaking_weights(z, kpos[None, :] < qpos[:, None])
    return jnp.einsum('nhtk,nkhd->nthd', a.astype(v_new.dtype), v_all)


def causal_short_conv(u, w, u_prev):
    T = u.shape[1]
    full = jnp.concatenate([u_prev, u], axis=1)
    out = sum(full[:, k:k + T] * w[k] for k in range(CONV_W))
    return out, full[:, full.shape[1] - (CONV_W - 1):]


def setup_inputs(seed: int = 0) -> dict:
    key = jax.random.key(seed)
    ks = jax.random.split(key, 13)
    n_pages = PAST_LEN // PAGE_SIZE
    n_used = DEC_BATCH * n_pages
    n_pool = n_used + n_used // 4
    perm = jax.random.permutation(ks[0], n_pool)
    page_table = perm[:n_used].reshape(DEC_BATCH, n_pages).astype(jnp.int32)
    x_prompt = jax.random.normal(ks[1], (BATCH, SEQ, D_MODEL), F32)
    x_sample = jax.random.normal(ks[2], (DEC_BATCH, DEC_SEQ, D_MODEL), F32)
    cache_k = jax.random.normal(ks[3], (N_ATTN_LAYERS, n_pool, PAGE_SIZE, N_HEADS, HEAD_DIM), F32)
    cache_v = jax.random.normal(ks[4], (N_ATTN_LAYERS, n_pool, PAGE_SIZE, N_HEADS, HEAD_DIM), F32)
    state_conv = jax.random.normal(ks[5], (N_CONV_LAYERS, DEC_BATCH, CONV_W - 1, D_MODEL), F32)
    w_in = jax.random.normal(ks[6], (DEPTH, D_MODEL, 4 * BRANCH), F32) * D_MODEL ** -0.5
    w_out = jax.random.normal(ks[7], (DEPTH, BRANCH, D_MODEL), F32) * BRANCH ** -0.5
    norm_g = 1.0 + 0.02 * jax.random.normal(ks[8], (DEPTH, D_MODEL), F32)
    conv_w = jax.random.normal(ks[9], (N_CONV_LAYERS, CONV_W, D_MODEL), F32) * CONV_W ** -0.5
    rel_bias = 0.1 * jax.random.normal(ks[10], (REL_BUCKETS, N_HEADS), F32)
    final_norm_g = 1.0 + 0.02 * jax.random.normal(ks[11], (D_MODEL,), F32)
    return {'x_prompt': x_prompt, 'x_sample': x_sample, 'cache_k': cache_k, 'cache_v': cache_v,
            'state_conv': state_conv, 'page_table': page_table, 'w_in': w_in, 'w_out': w_out,
            'norm_g': norm_g, 'conv_w': conv_w, 'rel_bias': rel_bias, 'final_norm_g': final_norm_g}


def reference(x_prompt, x_sample, cache_k, cache_v, state_conv, page_table,
              w_in, w_out, norm_g, conv_w, rel_bias, final_norm_g):
    hp, hs = x_prompt, x_sample
    k_p, v_p, k_s, v_s, conv_p, conv_s = [], [], [], [], [], []
    for i in range(DEPTH):
        kind = MIXER_OF_LAYER[i]
        p0p, p1p, p2p, gp = jnp.split(rmsnorm(hp, norm_g[i]) @ w_in[i], 4, axis=-1)
        p0s, p1s, p2s, gs = jnp.split(rmsnorm(hs, norm_g[i]) @ w_in[i], 4, axis=-1)
        if kind == SHORTCONV:
            c = CONV_LAYERS.index(i)
            zero_prev = jnp.zeros((hp.shape[0], CONV_W - 1, D_MODEL), hp.dtype)
            cvp, new_cp = causal_short_conv(p2p * p0p, conv_w[c], zero_prev)
            cvs, new_cs = causal_short_conv(p2s * p0s, conv_w[c], state_conv[c])
            op, os_ = p1p * cvp, p1s * cvs
            conv_p.append(new_cp)
            conv_s.append(new_cs)
        else:
            j = ATTN_LAYERS.index(i)
            qp, kp, vp = heads(p0p), heads(p1p), heads(p2p)
            qs, kss, vss = heads(p0s), heads(p1s), heads(p2s)
            if kind == MOBA:
                op = moba_prompt(qp, kp, vp, rel_bias)
                os_ = moba_sample(qs, kss, vss, cache_k, cache_v, j, page_table, rel_bias)
            else:
                op = stickbreak_prompt(qp, kp, vp)
                os_ = stickbreak_sample(qs, kss, vss, cache_k, cache_v, j, page_table)
            op = op.reshape(hp.shape[0], hp.shape[1], BRANCH)
            os_ = os_.reshape(hs.shape[0], hs.shape[1], BRANCH)
            k_p.append(kp)
            v_p.append(vp)
            k_s.append(kss)
            v_s.append(vss)
        hp = hp + (op * jax.nn.silu(gp)) @ w_out[i]
        hs = hs + (os_ * jax.nn.silu(gs)) @ w_out[i]
    y_prompt = rmsnorm(hp, final_norm_g)
    y_sample = rmsnorm(hs, final_norm_g)
    return (y_prompt, y_sample, jnp.stack(k_p), jnp.stack(v_p), jnp.stack(k_s), jnp.stack(v_s),
            jnp.stack(conv_p), jnp.stack(conv_s))
```

```python
import functools
import math

import jax
import jax.numpy as jnp
from jax import lax
from jax.experimental import pallas as pl
from jax.experimental.pallas import tpu as pltpu

F32 = jnp.float32
BF16 = jnp.bfloat16

HEAD_DIM = 128
MOBA_BLOCK = 256
MOBA_TOPK = 3
CONV_W = 3
REL_BUCKETS = 32
REL_EXACT = REL_BUCKETS // 2
REL_MAX_DIST = 128
NORM_EPS = 1e-6
PAGE_SIZE = 128
MOBA, SHORTCONV, STICKBREAK = 0, 1, 2
N_MIXERS = 3

LANES = 128
SUBLANES = 8
VMEM_LIMIT = 56 * 1024 * 1024

NEG = -1e30

NT_DIMS = (((1,), (1,)), ((), ()))


def _params(*sem):
    return pltpu.CompilerParams(dimension_semantics=sem, vmem_limit_bytes=VMEM_LIMIT)


def _inproj_kernel(x_ref, g_ref, w_ref, o_ref, xn_ref):
    @pl.when(pl.program_id(1) == 0)
    def _():
        x = x_ref[...]
        ms = jnp.mean(x * x, axis=-1, keepdims=True)
        xn_ref[...] = (x * lax.rsqrt(ms + NORM_EPS) * g_ref[...]).astype(BF16)

    o_ref[0] = jnp.dot(xn_ref[...], w_ref[...], preferred_element_type=F32)


def _inproj(x, g, w, *, tm, tn):
    m, d = x.shape
    n_out = w.shape[1]
    bn = n_out // 4
    per = bn // tn
    return pl.pallas_call(
        _inproj_kernel,
        grid=(m // tm, n_out // tn),
        in_specs=[
            pl.BlockSpec((tm, d), lambda i, n: (i, 0)),
            pl.BlockSpec((1, d), lambda i, n: (0, 0)),
            pl.BlockSpec((d, tn), lambda i, n: (0, n)),
        ],
        out_specs=pl.BlockSpec((1, tm, tn), lambda i, n: (n // per, i, n % per)),
        out_shape=jax.ShapeDtypeStruct((4, m, bn), F32),
        scratch_shapes=[pltpu.VMEM((tm, d), BF16)],
        compiler_params=_params("parallel", "arbitrary"),
        name="inproj",
    )(x, g, w)


def _outproj_kernel(h_ref, o_ref, gate_ref, w_ref, fg_ref, hn_ref, y_ref=None):
    gate = gate_ref[0]
    lhs = (o_ref[...] * (gate * jax.nn.sigmoid(gate))).astype(BF16)
    hn = h_ref[...] + jnp.dot(lhs, w_ref[...], preferred_element_type=F32)
    hn_ref[...] = hn
    if y_ref is not None:
        ms = jnp.mean(hn * hn, axis=-1, keepdims=True)
        y_ref[...] = hn * lax.rsqrt(ms + NORM_EPS) * fg_ref[...]


def _outproj(h, o, proj, w, fg, *, tm, final):
    m, d = h.shape
    bn = o.shape[1]
    row = pl.BlockSpec((tm, d), lambda i: (i, 0))
    out_shape = [jax.ShapeDtypeStruct((m, d), F32)]
    out_specs = [row]
    if final:
        out_shape.append(jax.ShapeDtypeStruct((m, d), F32))
        out_specs.append(row)
    res = pl.pallas_call(
        _outproj_kernel,
        grid=(m // tm,),
        in_specs=[
            row,
            pl.BlockSpec((tm, bn), lambda i: (i, 0)),
            pl.BlockSpec((1, tm, bn), lambda i: (3, i, 0)),
            pl.BlockSpec((bn, d), lambda i: (0, 0)),
            pl.BlockSpec((1, d), lambda i: (0, 0)),
        ],
        out_specs=out_specs,
        out_shape=out_shape,
        compiler_params=_params("parallel"),
        name="outproj_final" if final else "outproj",
    )(h, o, proj, w, fg)
    return res if final else res[0]


def _rel_bucket(dist):
    n = jnp.maximum(dist, 0)
    nf = jnp.maximum(n, REL_EXACT).astype(F32)
    large = REL_EXACT + (jnp.log(nf / REL_EXACT)
                         * ((REL_BUCKETS - REL_EXACT) / math.log(REL_MAX_DIST / REL_EXACT))).astype(jnp.int32)
    return jnp.where(n < REL_EXACT, n, jnp.minimum(large, REL_BUCKETS - 1))


def _bias_kernel(rb_ref, o_ref, *, offsets):
    h = pl.program_id(0)
    tq, tk = o_ref.shape[2], o_ref.shape[3]
    rel = (lax.broadcasted_iota(jnp.int32, (tq, tk), 0)
           - lax.broadcasted_iota(jnp.int32, (tq, tk), 1))
    for idx, off in enumerate(offsets):
        bucket = _rel_bucket(rel + off)
        acc = jnp.zeros((tq, tk), F32)
        for b in range(REL_BUCKETS):
            acc = jnp.where(bucket == b, rb_ref[b, h], acc)
        o_ref[0, idx] = acc


def _bias_tiles(rel_bias, offsets, tq, tk):
    n_heads = rel_bias.shape[1]
    return pl.pallas_call(
        functools.partial(_bias_kernel, offsets=tuple(offsets)),
        grid=(n_heads,),
        in_specs=[pl.BlockSpec(memory_space=pltpu.SMEM)],
        out_specs=pl.BlockSpec((1, len(offsets), tq, tk), lambda h: (h, 0, 0, 0)),
        out_shape=jax.ShapeDtypeStruct((n_heads, len(offsets), tq, tk), F32),
        compiler_params=_params("parallel"),
        name="bias_tiles",
    )(rel_bias)


def _topk_mask(gate, valid, n_sel):
    lane = lax.broadcasted_iota(jnp.int32, gate.shape, 1).astype(F32)
    g = jnp.where(valid, gate, -jnp.inf)
    sel = jnp.zeros(gate.shape, F32)
    for _ in range(n_sel):
        mx = jnp.max(g, axis=1, keepdims=True)
        first = jnp.min(jnp.where(g == mx, lane, float(gate.shape[1])), axis=1, keepdims=True)
        pick = lane == first
        sel = jnp.where(pick, 1.0, sel)
        g = jnp.where(pick, -jnp.inf, g)
    return jnp.logical_and(sel > 0.0, valid)


def _moba_prompt_kernel(q_ref, k_ref, v_ref, bias_ref, o_ref, kbf, vbf, kmean, *, n_blk, scale):
    qb = pl.program_id(2)
    blk = MOBA_BLOCK

    @pl.when(qb == 0)
    def _():
        kbf[...] = k_ref[0, 0].astype(BF16)
        vbf[...] = v_ref[0, 0].astype(BF16)
        kmean[...] = jnp.zeros(kmean.shape, F32)
        for n in range(n_blk):
            kmean[n:n + 1, :] = jnp.sum(k_ref[0, 0, n * blk:(n + 1) * blk, :], axis=0,
                                        keepdims=True) * (1.0 / blk)

    q = q_ref[0]
    qbf = q.astype(BF16)
    gate = lax.dot_general(q, kmean[...], NT_DIMS, precision=lax.Precision.HIGHEST,
                           preferred_element_type=F32)
    lane = lax.broadcasted_iota(jnp.int32, gate.shape, 1)
    sel = _topk_mask(gate, lane < qb, min(MOBA_TOPK, n_blk - 1))
    self32 = jnp.where(sel, 1.0, 0.0)

    def past_block(n, carry):
        m, l, acc = carry
        start = pl.multiple_of(n * blk, blk)
        s = lax.dot_general(qbf, kbf[pl.ds(start, blk), :], NT_DIMS, preferred_element_type=F32)
        s = s * scale + bias_ref[0, jnp.minimum(qb - n, 2)]
        chosen = jnp.max(jnp.where(lane == n, self32, 0.0), axis=1, keepdims=True) > 0.0
        s = jnp.where(chosen, s, NEG)
        m_new = jnp.maximum(m, jnp.max(s, axis=1, keepdims=True))
        alpha = jnp.exp(m - m_new)
        p = jnp.exp(s - m_new)
        l = alpha * l + jnp.sum(p, axis=1, keepdims=True)
        acc = alpha * acc + jnp.dot(p.astype(BF16), vbf[pl.ds(start, blk), :],
                                    preferred_element_type=F32)
        return m_new, l, acc

    init = (jnp.full((blk, 1), NEG, F32), jnp.zeros((blk, 1), F32), jnp.zeros((blk, HEAD_DIM), F32))
    m, l, acc = lax.fori_loop(0, qb, past_block, init)

    start = pl.multiple_of(qb * blk, blk)
    s = lax.dot_general(qbf, kbf[pl.ds(start, blk), :], NT_DIMS, preferred_element_type=F32)
    s = s * scale + bias_ref[0, 0]
    row = lax.broadcasted_iota(jnp.int32, s.shape, 0)
    col = lax.broadcasted_iota(jnp.int32, s.shape, 1)
    s = jnp.where(col <= row, s, NEG)
    m_new = jnp.maximum(m, jnp.max(s, axis=1, keepdims=True))
    alpha = jnp.exp(m - m_new)
    p = jnp.exp(s - m_new)
    l = alpha * l + jnp.sum(p, axis=1, keepdims=True)
    acc = alpha * acc + jnp.dot(p.astype(BF16), vbf[pl.ds(start, blk), :], preferred_element_type=F32)
    o_ref[0] = acc / l


def _moba_prompt(proj, bias, nb, seq):
    width = proj.shape[2]
    n_heads = width // HEAD_DIM
    n_blk = seq // MOBA_BLOCK
    proj4 = proj.reshape(4, nb, seq, width)
    return pl.pallas_call(
        functools.partial(_moba_prompt_kernel, n_blk=n_blk, scale=HEAD_DIM ** -0.5),
        grid=(nb, n_heads, n_blk),
        in_specs=[
            pl.BlockSpec((1, MOBA_BLOCK, HEAD_DIM), lambda b, h, i: (0, b * n_blk + i, h)),
            pl.BlockSpec((1, 1, seq, HEAD_DIM), lambda b, h, i: (1, b, 0, h)),
            pl.BlockSpec((1, 1, seq, HEAD_DIM), lambda b, h, i: (2, b, 0, h)),
            pl.BlockSpec((1, 3, MOBA_BLOCK, MOBA_BLOCK), lambda b, h, i: (h, 0, 0, 0)),
        ],
        out_specs=pl.BlockSpec((1, MOBA_BLOCK, HEAD_DIM), lambda b, h, i: (b, i, h)),
        out_shape=jax.ShapeDtypeStruct((nb, seq, width), F32),
        scratch_shapes=[pltpu.VMEM((seq, HEAD_DIM), BF16), pltpu.VMEM((seq, HEAD_DIM), BF16),
                        pltpu.VMEM((LANES, HEAD_DIM), F32)],
        compiler_params=_params("parallel", "parallel", "arbitrary"),
        name="moba_prompt",
    )(proj, proj4, proj4, bias)


def _log_keep(z):
    return -(jnp.maximum(z, 0.0) + jnp.log1p(jnp.exp(-jnp.abs(z))))


def _suffix_sum(lk, tri):
    hi = lk.astype(BF16)
    lo = (lk - hi.astype(F32)).astype(BF16)
    return (jnp.dot(hi, tri, preferred_element_type=F32)
            + jnp.dot(lo, tri, preferred_element_type=F32))


def _tri(n):
    r = lax.broadcasted_iota(jnp.int32, (n, n), 0)
    c = lax.broadcasted_iota(jnp.int32, (n, n), 1)
    return jnp.where(r >= c, 1.0, 0.0).astype(BF16)


def _stick_tile(z, v_bf, tri, carry, visible=None):
    lk = _log_keep(z)
    if visible is not None:
        lk = jnp.where(visible, lk, 0.0)
    suffix = _suffix_sum(lk, tri)
    a = jnp.exp(z + suffix + carry)
    if visible is not None:
        a = jnp.where(visible, a, 0.0)
    return jnp.dot(a.astype(BF16), v_bf, preferred_element_type=F32), carry + suffix[:, 0:1]


def _stick_prompt_kernel(q_ref, k_ref, v_ref, o_ref, kbf, vbf, *, tq, scale):
    qt = pl.program_id(2)

    @pl.when(qt == 0)
    def _():
        kbf[...] = k_ref[0, 0].astype(BF16)
        vbf[...] = v_ref[0, 0].astype(BF16)

    qbf = q_ref[0].astype(BF16)
    tri = _tri(tq)

    def logits(n):
        start = pl.multiple_of(n * tq, tq)
        z = lax.dot_general(qbf, kbf[pl.ds(start, tq), :], NT_DIMS, preferred_element_type=F32)
        return z * scale, vbf[pl.ds(start, tq), :]

    z, v_bf = logits(qt)
    row = lax.broadcasted_iota(jnp.int32, z.shape, 0)
    col = lax.broadcasted_iota(jnp.int32, z.shape, 1)
    acc, carry = _stick_tile(z, v_bf, tri, jnp.zeros((tq, 1), F32), visible=col < row)

    def earlier_tile(i, state):
        acc, carry = state
        z, v_bf = logits(qt - 1 - i)
        out, carry = _stick_tile(z, v_bf, tri, carry)
        return acc + out, carry

    acc, _ = lax.fori_loop(0, qt, earlier_tile, (acc, carry))
    o_ref[0] = acc


def _stick_prompt(proj, nb, seq, *, tq):
    width = proj.shape[2]
    n_heads = width // HEAD_DIM
    n_qt = seq // tq
    proj4 = proj.reshape(4, nb, seq, width)
    return pl.pallas_call(
        functools.partial(_stick_prompt_kernel, tq=tq, scale=HEAD_DIM ** -0.5),
        grid=(nb, n_heads, n_qt),
        in_specs=[
            pl.BlockSpec((1, tq, HEAD_DIM), lambda b, h, i: (0, b * n_qt + i, h)),
            pl.BlockSpec((1, 1, seq, HEAD_DIM), lambda b, h, i: (1, b, 0, h)),
            pl.BlockSpec((1, 1, seq, HEAD_DIM), lambda b, h, i: (2, b, 0, h)),
        ],
        out_specs=pl.BlockSpec((1, tq, HEAD_DIM), lambda b, h, i: (b, i, h)),
        out_shape=jax.ShapeDtypeStruct((nb, seq, width), F32),
        scratch_shapes=[pltpu.VMEM((seq, HEAD_DIM), BF16), pltpu.VMEM((seq, HEAD_DIM), BF16)],
        compiler_params=_params("parallel", "parallel", "arbitrary"),
        name="stick_prompt",
    )(proj, proj4, proj4)


def _block_diag_q(q, n_heads):
    t, width = q.shape
    rows = n_heads * t
    rep = jnp.concatenate([q] * n_heads, axis=0)
    rhead = lax.broadcasted_iota(jnp.int32, (rows, width), 0) // t
    chead = lax.broadcasted_iota(jnp.int32, (rows, width), 1) // HEAD_DIM
    return jnp.where(rhead == chead, rep, 0.0)


def _head_diag(full, n_heads, t):
    return jnp.concatenate(
        [full[h * t:(h + 1) * t, h * HEAD_DIM:(h + 1) * HEAD_DIM] for h in range(n_heads)], axis=0)


def _page_matrix(page_ref, n_heads):
    return jnp.concatenate(
        [page_ref[0, 0, pl.ds(h, PAGE_SIZE, stride=n_heads), :].astype(BF16) for h in range(n_heads)],
        axis=1)


def _rows_to_tokens(o_ref, rows, n_heads, t):
    for h in range(n_heads):
        o_ref[0, :, h * HEAD_DIM:(h + 1) * HEAD_DIM] = rows[h * t:(h + 1) * t, :]


def _stick_sample_kernel(pt_ref, q_ref, kn_ref, vn_ref, *refs, n_heads, t, tp, scale):
    k_pages = refs[:tp]
    v_pages = refs[tp:2 * tp]
    o_ref, qbd, acc_ref, carry_ref = refs[2 * tp:]
    step = pl.program_id(1)
    rows = n_heads * t
    tri = _tri(PAGE_SIZE)

    @pl.when(step == 0)
    def _():
        qbd[...] = _block_diag_q(q_ref[0], n_heads).astype(BF16)
        z = lax.dot_general(qbd[...], kn_ref[0].astype(BF16), NT_DIMS,
                            preferred_element_type=F32) * scale
        tok = lax.broadcasted_iota(jnp.int32, z.shape, 0) % t
        key = lax.broadcasted_iota(jnp.int32, z.shape, 1)
        lk = jnp.where(key < tok, _log_keep(z), 0.0)
        suffix = _suffix_sum(lk, tri)
        a = jnp.where(key < tok, jnp.exp(z + suffix), 0.0)
        full = jnp.dot(a.astype(BF16), vn_ref[0].astype(BF16), preferred_element_type=F32)
        acc_ref[...] = _head_diag(full, n_heads, t)
        carry_ref[...] = suffix[:, 0:1]

    acc = acc_ref[...]
    carry = carry_ref[...]
    for p in reversed(range(tp)):
        z = lax.dot_general(qbd[...], _page_matrix(k_pages[p], n_heads), NT_DIMS,
                            preferred_element_type=F32) * scale
        lk = _log_keep(z)
        suffix = _suffix_sum(lk, tri)
        a = jnp.exp(z + suffix + carry)
        full = jnp.dot(a.astype(BF16), _page_matrix(v_pages[p], n_heads),
                       preferred_element_type=F32)
        acc = acc + _head_diag(full, n_heads, t)
        carry = carry + suffix[:, 0:1]
    acc_ref[...] = acc
    carry_ref[...] = carry

    @pl.when(step == pl.num_programs(1) - 1)
    def _():
        _rows_to_tokens(o_ref, acc, n_heads, t)


def _page_specs(layer, tp, n_steps, width, reverse):
    specs = []
    for p in range(tp):
        if reverse:
            fn = lambda n, s, pt, p=p: (layer, pt[n, (n_steps - 1 - s) * tp + p], 0, 0)
        else:
            fn = lambda n, s, pt, p=p: (layer, pt[n, s * tp + p], 0, 0)
        specs.append(pl.BlockSpec((1, 1, PAGE_SIZE * (width // HEAD_DIM), HEAD_DIM), fn))
    return specs


def _pad_new(x):
    return jnp.pad(x, ((0, 0), (0, PAGE_SIZE - x.shape[1]), (0, 0)))


def _stick_sample(proj, cache_k, cache_v, layer, page_table, n, t, *, tp):
    width = proj.shape[2]
    n_heads = width // HEAD_DIM
    rows = n_heads * t
    n_pages = page_table.shape[1]
    n_steps = n_pages // tp
    proj4 = proj.reshape(4, n, t, width)
    k_new = _pad_new(proj4[1])
    v_new = _pad_new(proj4[2])
    tok = lambda s: pl.BlockSpec((1,) + s, lambda i, j, pt: (i, 0, 0))
    pages = _page_specs(layer, tp, n_steps, width, reverse=True)
    grid_spec = pltpu.PrefetchScalarGridSpec(
        num_scalar_prefetch=1,
        grid=(n, n_steps),
        in_specs=[tok((t, width)), tok((PAGE_SIZE, width)), tok((PAGE_SIZE, width))] + pages + pages,
        out_specs=tok((t, width)),
        scratch_shapes=[pltpu.VMEM((rows, width), BF16), pltpu.VMEM((rows, HEAD_DIM), F32),
                        pltpu.VMEM((rows, 1), F32)],
    )
    return pl.pallas_call(
        functools.partial(_stick_sample_kernel, n_heads=n_heads, t=t, tp=tp, scale=HEAD_DIM ** -0.5),
        grid_spec=grid_spec,
        out_shape=jax.ShapeDtypeStruct((n, t, width), F32),
        compiler_params=_params("parallel", "arbitrary"),
        name="stick_sample",
    )(page_table, proj4[0], k_new, v_new, *([cache_k] * tp), *([cache_v] * tp))


def _moba_sample_kernel(pt_ref, q_ref, kn_ref, vn_ref, bown_ref, blast_ref, bfar_ref, *refs,
                        n_heads, t, tp, n_blk, scale):
    k_pages = refs[:tp]
    v_pages = refs[tp:2 * tp]
    o_ref, qbd, kmean, o_blk, m_blk, l_blk = refs[2 * tp:]
    step = pl.program_id(1)
    rows = n_heads * t
    ppb = MOBA_BLOCK // PAGE_SIZE
    lane = lax.broadcasted_iota(jnp.int32, (rows, LANES), 1)

    @pl.when(step == 0)
    def _():
        qbd[...] = _block_diag_q(q_ref[0], n_heads).astype(BF16)
        kmean[...] = jnp.zeros(kmean.shape, F32)
        m_blk[...] = jnp.zeros(m_blk.shape, F32)
        l_blk[...] = jnp.zeros(l_blk.shape, F32)

    for j in range(tp // ppb):
        b = step * (tp // ppb) + j
        ksum = jnp.zeros((n_heads, HEAD_DIM), F32)
        s_parts = []
        for p in range(ppb):
            kp_ref = k_pages[j * ppb + p]
            ksum = ksum + jnp.sum(kp_ref[0, 0].reshape(PAGE_SIZE, n_heads, HEAD_DIM), axis=0)
            s = lax.dot_general(qbd[...], _page_matrix(kp_ref, n_heads), NT_DIMS,
                                preferred_element_type=F32) * scale
            near = blast_ref[:, p * PAGE_SIZE:(p + 1) * PAGE_SIZE]
            s_parts.append(s + jnp.where(b == n_blk - 1, near, bfar_ref[...]))
        for h in range(n_heads):
            kmean[h, pl.ds(b, 1), :] = ksum[h:h + 1, :] * (1.0 / MOBA_BLOCK)
        mb = s_parts[0].max(axis=1, keepdims=True)
        for s in s_parts[1:]:
            mb = jnp.maximum(mb, s.max(axis=1, keepdims=True))
        lb = jnp.zeros((rows, 1), F32)
        full = jnp.zeros((rows, qbd.shape[1]), F32)
        for p, s in enumerate(s_parts):
            pexp = jnp.exp(s - mb)
            lb = lb + jnp.sum(pexp, axis=1, keepdims=True)
            full = full + jnp.dot(pexp.astype(BF16), _page_matrix(v_pages[j * ppb + p], n_heads),
                                  preferred_element_type=F32)
        o_blk[b] = _head_diag(full, n_heads, t)
        m_blk[...] = jnp.where(lane == b, mb, m_blk[...])
        l_blk[...] = jnp.where(lane == b, lb, l_blk[...])

    @pl.when(step == pl.num_programs(1) - 1)
    def _():
        q = q_ref[0]
        gate = jnp.concatenate(
            [lax.dot_general(q[:, h * HEAD_DIM:(h + 1) * HEAD_DIM], kmean[h], NT_DIMS,
                             precision=lax.Precision.HIGHEST, preferred_element_type=F32)
             for h in range(n_heads)], axis=0)
        sel = _topk_mask(gate, lane < n_blk, min(MOBA_TOPK, n_blk))
        s = lax.dot_general(qbd[...], kn_ref[0].astype(BF16), NT_DIMS,
                            preferred_element_type=F32) * scale + bown_ref[...]
        tok = lax.broadcasted_iota(jnp.int32, s.shape, 0) % t
        key = lax.broadcasted_iota(jnp.int32, s.shape, 1)
        s = jnp.where(key <= tok, s, NEG)
        m_own = jnp.max(s, axis=1, keepdims=True)
        p_own = jnp.exp(s - m_own)
        l_own = jnp.sum(p_own, axis=1, keepdims=True)
        o_own = _head_diag(jnp.dot(p_own.astype(BF16), vn_ref[0].astype(BF16),
                                   preferred_element_type=F32), n_heads, t)
        m_all = jnp.maximum(m_own, jnp.max(jnp.where(sel, m_blk[...], NEG), axis=1, keepdims=True))
        w = jnp.where(sel, jnp.exp(m_blk[...] - m_all), 0.0)
        w_own = jnp.exp(m_own - m_all)
        denom = w_own * l_own + jnp.sum(w * l_blk[...], axis=1, keepdims=True)

        def add_block(b, acc):
            wb = jnp.sum(jnp.where(lane == b, w, 0.0), axis=1, keepdims=True)
            return acc + wb * o_blk[b]

        acc = lax.fori_loop(0, n_blk, add_block, w_own * o_own)
        _rows_to_tokens(o_ref, acc / denom, n_heads, t)


def _moba_sample(proj, cache_k, cache_v, layer, page_table, bias_own, bias_last, bias_far,
                 n, t, *, tp):
    width = proj.shape[2]
    n_heads = width // HEAD_DIM
    rows = n_heads * t
    n_pages = page_table.shape[1]
    n_steps = n_pages // tp
    n_blk = n_pages * PAGE_SIZE // MOBA_BLOCK
    proj4 = proj.reshape(4, n, t, width)
    k_new = _pad_new(proj4[1])
    v_new = _pad_new(proj4[2])
    tok = lambda s: pl.BlockSpec((1,) + s, lambda i, j, pt: (i, 0, 0))
    whole = lambda a: pl.BlockSpec(a.shape, lambda i, j, pt: (0,) * a.ndim)
    pages = _page_specs(layer, tp, n_steps, width, reverse=False)
    grid_spec = pltpu.PrefetchScalarGridSpec(
        num_scalar_prefetch=1,
        grid=(n, n_steps),
        in_specs=[tok((t, width)), tok((PAGE_SIZE, width)), tok((PAGE_SIZE, width)),
                  whole(bias_own), whole(bias_last), whole(bias_far)] + pages + pages,
        out_specs=tok((t, width)),
        scratch_shapes=[pltpu.VMEM((rows, width), BF16), pltpu.VMEM((n_heads, LANES, HEAD_DIM), F32),
                        pltpu.VMEM((n_blk, rows, HEAD_DIM), F32),
                        pltpu.VMEM((rows, LANES), F32), pltpu.VMEM((rows, LANES), F32)],
    )
    return pl.pallas_call(
        functools.partial(_moba_sample_kernel, n_heads=n_heads, t=t, tp=tp, n_blk=n_blk,
                          scale=HEAD_DIM ** -0.5),
        grid_spec=grid_spec,
        out_shape=jax.ShapeDtypeStruct((n, t, width), F32),
        compiler_params=_params("parallel", "arbitrary"),
        name="moba_sample",
    )(page_table, proj4[0], k_new, v_new, bias_own, bias_last, bias_far,
      *([cache_k] * tp), *([cache_v] * tp))


def _conv_kernel(p0_ref, p1_ref, p2_ref, h0_ref, h2_ref, w_ref, o_ref, tail_ref, *, halo_is_state):
    u = p2_ref[0] * p0_ref[0]
    if halo_is_state:
        halo = h0_ref[0]
    else:
        halo = jnp.where(pl.program_id(1) == 0, 0.0, h2_ref[0] * h0_ref[0])
    prev1 = halo[SUBLANES - 1:SUBLANES, :]
    prev2 = halo[SUBLANES - 2:SUBLANES - 1, :]
    row = lax.broadcasted_iota(jnp.int32, u.shape, 0)
    u1 = jnp.where(row == 0, prev1, pltpu.roll(u, 1, axis=0))
    u2 = jnp.where(row == 0, prev2, jnp.where(row == 1, prev1, pltpu.roll(u, 2, axis=0)))
    w = w_ref[...]
    o_ref[0] = p1_ref[0] * (u2 * w[0:1, :] + u1 * w[1:2, :] + u * w[2:3, :])
    tail_ref[0] = u[u.shape[0] - SUBLANES:, :]


def _short_conv(proj, halo, w, nb, seq, *, ts):
    width = proj.shape[2]
    n_t = seq // ts
    proj4 = proj.reshape(4, nb, seq, width)
    cur = lambda i: pl.BlockSpec((1, 1, ts, width), lambda b, s: (i, b, s, 0))
    if halo is None:
        per = ts // SUBLANES
        prev = lambda i: pl.BlockSpec((1, 1, SUBLANES, width),
                                      lambda b, s: (i, b, jnp.maximum(s * per - 1, 0), 0))
        halo_specs, halo_args = [prev(0), prev(2)], [proj4, proj4]
    else:
        spec = pl.BlockSpec((1, SUBLANES, width), lambda b, s: (b, 0, 0))
        halo_specs, halo_args = [spec, spec], [halo, halo]

    def kern(p0, p1, p2, h0, h2, w_ref, o_ref, tail_ref):
        if halo is None:
            _conv_kernel(p0.at[0], p1.at[0], p2.at[0], h0.at[0], h2.at[0], w_ref, o_ref, tail_ref,
                         halo_is_state=False)
        else:
            _conv_kernel(p0.at[0], p1.at[0], p2.at[0], h0, h2, w_ref, o_ref, tail_ref,
                         halo_is_state=True)

    return pl.pallas_call(
        kern,
        grid=(nb, n_t),
        in_specs=[cur(0), cur(1), cur(2)] + halo_specs
                 + [pl.BlockSpec((CONV_W, width), lambda b, s: (0, 0))],
        out_specs=[pl.BlockSpec((1, ts, width), lambda b, s: (b, s, 0)),
                   pl.BlockSpec((1, SUBLANES, width), lambda b, s: (b, 0, 0))],
        out_shape=[jax.ShapeDtypeStruct((nb, seq, width), F32),
                   jax.ShapeDtypeStruct((nb, SUBLANES, width), F32)],
        compiler_params=_params("parallel", "arbitrary"),
        name="short_conv",
    )(proj4, proj4, proj4, *halo_args, w)


def kernel(x_prompt, x_sample, cache_k, cache_v, state_conv, page_table, w_in, w_out, norm_g,
           conv_w, rel_bias, final_norm_g):
    nb, seq, d = x_prompt.shape
    ns, t, _ = x_sample.shape
    depth = w_in.shape[0]
    n_heads = rel_bias.shape[1]
    width = n_heads * HEAD_DIM
    n_pool = cache_k.shape[1]
    past = page_table.shape[1] * PAGE_SIZE
    assert past % MOBA_BLOCK == 0 and seq % MOBA_BLOCK == 0 and seq // MOBA_BLOCK > MOBA_TOPK
    assert n_heads * t == LANES and t <= SUBLANES

    ck = cache_k.reshape(cache_k.shape[0], n_pool, PAGE_SIZE * n_heads, HEAD_DIM)
    cv = cache_v.reshape(cache_v.shape[0], n_pool, PAGE_SIZE * n_heads, HEAD_DIM)
    w_in_bf = w_in.astype(BF16)
    w_out_bf = w_out.astype(BF16)
    fg = final_norm_g.reshape(1, d)

    bias_p = _bias_tiles(rel_bias, (0, MOBA_BLOCK, 2 * MOBA_BLOCK), MOBA_BLOCK, MOBA_BLOCK)
    bias_s = _bias_tiles(rel_bias, (0, MOBA_BLOCK, 2 * MOBA_BLOCK), t, MOBA_BLOCK)
    rows = n_heads * t
    bias_own = bias_s[:, 0, :, :PAGE_SIZE].reshape(rows, PAGE_SIZE)
    bias_last = bias_s[:, 1].reshape(rows, MOBA_BLOCK)
    bias_far = bias_s[:, 2, :, :1].reshape(rows, 1)

    hp = x_prompt.reshape(nb * seq, d)
    hs = x_sample.reshape(ns * t, d)
    k_p, v_p, k_s, v_s, conv_p, conv_s = [], [], [], [], [], []
    attn = 0
    conv = 0
    for i in range(depth):
        kind = i % N_MIXERS
        g = norm_g[i].reshape(1, d)
        pp = _inproj(hp, g, w_in_bf[i], tm=512, tn=1024)
        ps = _inproj(hs, g, w_in_bf[i], tm=ns * t, tn=1024)
        if kind == SHORTCONV:
            op, tail_p = _short_conv(pp, None, conv_w[conv], nb, seq, ts=512)
            halo = jnp.pad(state_conv[conv], ((0, 0), (SUBLANES - (CONV_W - 1), 0), (0, 0)))
            os_, tail_s = _short_conv(ps, halo, conv_w[conv], ns, t, ts=t)
            conv_p.append(tail_p[:, SUBLANES - (CONV_W - 1):])
            conv_s.append(tail_s[:, SUBLANES - (CONV_W - 1):])
            conv += 1
        else:
            if kind == MOBA:
                op = _moba_prompt(pp, bias_p, nb, seq)
                os_ = _moba_sample(ps, ck, cv, attn, page_table, bias_own, bias_last, bias_far,
                                   ns, t, tp=4)
            else:
                op = _stick_prompt(pp, nb, seq, tq=256)
                os_ = _stick_sample(ps, ck, cv, attn, page_table, ns, t, tp=4)
            k_p.append(pp[1])
            v_p.append(pp[2])
            k_s.append(ps[1])
            v_s.append(ps[2])
            attn += 1
        final = i == depth - 1
        op = op.reshape(nb * seq, width)
        os_ = os_.reshape(ns * t, width)
        rp = _outproj(hp, op, pp, w_out_bf[i], fg, tm=256, final=final)
        rs = _outproj(hs, os_, ps, w_out_bf[i], fg, tm=ns * t, final=final)
        if final:
            (hp, y_p), (hs, y_s) = rp, rs
        else:
            hp, hs = rp, rs

    shape_p = (len(k_p), nb, seq, n_heads, HEAD_DIM)
    shape_s = (len(k_s), ns, t, n_heads, HEAD_DIM)
    return (y_p.reshape(nb, seq, d), y_s.reshape(ns, t, d),
            jnp.stack(k_p).reshape(shape_p), jnp.stack(v_p).reshape(shape_p),
            jnp.stack(k_s).reshape(shape_s), jnp.stack(v_s).reshape(shape_s),
            jnp.stack(conv_p), jnp.stack(conv_s))
```
